```python
import math
import jax, jax.numpy as jnp
from jax import lax
import numpy as np

D_MODEL = 1024
BATCH = 16
SEQ = 2048
DEPTH = 1

D_MIX = D_MODEL
HEAD_DIM = 64
D_ATTN = D_MIX // 2
N_Q_HEADS = D_ATTN // HEAD_DIM
N_KV_HEADS = 2
Q_PER_KV = N_Q_HEADS // N_KV_HEADS
WINDOW = 128
BLOCK = WINDOW
D_SSM = D_MIX - D_ATTN
SSM_GROUP = 16
N_SSM_GROUPS = D_SSM // SSM_GROUP
STATE = 64
D_KV = N_KV_HEADS * HEAD_DIM
D_IN = D_ATTN + 2 * D_KV + D_SSM
D_FF = ((8 * D_MODEL // 3 + 127) // 128) * 128
N_MOD = 9
EPS = 1e-6

kernel_name = "hybrid_swa_sinks_s5_macaron_sandwich_adaln"


def rms_norm(x, g):
    xf = x.astype(jnp.float32)
    y = xf * lax.rsqrt(jnp.mean(xf * xf, axis=-1, keepdims=True) + EPS)
    return (y * g.astype(jnp.float32)).astype(x.dtype)


def modulate(h, shift, scale):
    return h * (1.0 + scale[:, None, :]) + shift[:, None, :]


def swiglu(h, w_gate, w_up, w_down):
    return (jax.nn.silu(h @ w_gate) * (h @ w_up)) @ w_down


def alibi_slopes():
    i = jnp.arange(1, N_Q_HEADS + 1, dtype=jnp.float32)
    return jnp.exp2(-8.0 * i / N_Q_HEADS)


def sliding_window_attention(q, k, v, sinks):
    b, s = q.shape[0], q.shape[1]
    nb = s // BLOCK
    qb = q.reshape(b, nb, BLOCK, N_KV_HEADS, Q_PER_KV, HEAD_DIM)
    kb = k.reshape(b, nb, BLOCK, N_KV_HEADS, HEAD_DIM)
    vb = v.reshape(b, nb, BLOCK, N_KV_HEADS, HEAD_DIM)
    pad = ((0, 0), (1, 0), (0, 0), (0, 0), (0, 0))
    kk = jnp.concatenate([jnp.pad(kb[:, :-1], pad), kb], axis=2)
    vv = jnp.concatenate([jnp.pad(vb[:, :-1], pad), vb], axis=2)

    scores = jnp.einsum('bnqkgd,bnjkd->bnkgqj', qb, kk).astype(jnp.float32) * (HEAD_DIM ** -0.5)

    qi = jnp.arange(BLOCK)[:, None]
    kj = jnp.arange(2 * BLOCK)[None, :]
    dist = qi + BLOCK - kj
    key_pos = jnp.arange(nb)[:, None, None] * BLOCK + kj[None] - BLOCK
    valid = (dist[None] >= 0) & (dist[None] < WINDOW) & (key_pos >= 0)

    slopes = alibi_slopes().reshape(N_KV_HEADS, Q_PER_KV)
    bias = -slopes[:, :, None, None] * dist.astype(jnp.float32)[None, None]
    scores = jnp.where(valid[None, :, None, None], scores + bias, -jnp.inf)

    sink = sinks.astype(jnp.float32).reshape(N_KV_HEADS, Q_PER_KV)[None, None, :, :, None, None]
    m = jnp.maximum(jnp.max(scores, axis=-1, keepdims=True), sink)
    p = jnp.exp(scores - m)
    p = p / (jnp.sum(p, axis=-1, keepdims=True) + jnp.exp(sink - m))
    out = jnp.einsum('bnkgqj,bnjkd->bnqkgd', p.astype(v.dtype), vv)
    return out.reshape(b, s, N_Q_HEADS * HEAD_DIM)


def s5_ssm(u, a_re, a_im, log_step, b_re, b_im, c_re, c_im, d_skip, w_glu, b_glu):
    f32 = jnp.float32
    bsz, s = u.shape[0], u.shape[1]
    uf = u.astype(f32).reshape(bsz, s, N_SSM_GROUPS, SSM_GROUP)
    lam = lax.complex(a_re.astype(f32), a_im.astype(f32))
    dt = jnp.exp(log_step.astype(f32))[:, None]
    lam_bar = jnp.exp(lam * dt)
    b_mat = lax.complex(b_re.astype(f32), b_im.astype(f32))
    b_bar = ((lam_bar - 1.0) / lam)[:, :, None] * b_mat
    bu = jnp.einsum('bsgh,gph->bsgp', uf.astype(jnp.complex64), b_bar)
    a = jnp.broadcast_to(lam_bar, bu.shape)

    def combine(left, right):
        a_l, b_l = left
        a_r, b_r = right
        return a_r * a_l, a_r * b_l + b_r

    _, states = lax.associative_scan(combine, (a, bu), axis=1)
    c_mat = lax.complex(c_re.astype(f32), c_im.astype(f32))
    y = jnp.real(jnp.einsum('bsgp,ghp->bsgh', states, c_mat))
    y = y + uf * d_skip.astype(f32).reshape(N_SSM_GROUPS, SSM_GROUP)
    y = jax.nn.gelu(y.reshape(bsz, s, D_SSM))
    y = y * jax.nn.sigmoid(y @ w_glu.astype(f32) + b_glu.astype(f32))
    return y.astype(u.dtype)


def setup_inputs(seed: int = 0) -> dict:
    key = jax.random.key(seed)
    ks = jax.random.split(key, 40)
    f32 = jnp.float32
    L = DEPTH
    nrm = lambda k, shape, s: jax.random.normal(k, shape, f32) * s
    gain = lambda k: 1.0 + 0.05 * jax.random.normal(k, (L, D_MODEL), f32)
    n_idx = jnp.arange(STATE, dtype=f32)
    return {
        "x": jax.random.normal(ks[0], (BATCH, SEQ, D_MODEL), f32),
        "c": jax.random.normal(ks[1], (BATCH, D_MODEL), f32),
        "w_ada": nrm(ks[2], (L, D_MODEL, N_MOD * D_MODEL), 0.5 * D_MODEL ** -0.5),
        "b_ada": nrm(ks[3], (L, N_MOD * D_MODEL), 0.02),
        "g_pre_ff1": gain(ks[4]),
        "g_post_ff1": gain(ks[5]),
        "w1_gate": nrm(ks[6], (L, D_MODEL, D_FF), D_MODEL ** -0.5),
        "w1_up": nrm(ks[7], (L, D_MODEL, D_FF), D_MODEL ** -0.5),
        "w1_down": nrm(ks[8], (L, D_FF, D_MODEL), D_FF ** -0.5),
        "g_pre_mix": gain(ks[9]),
        "g_post_mix": gain(ks[10]),
        "w_in": nrm(ks[11], (L, D_MODEL, D_IN), D_MODEL ** -0.5),
        "attn_sinks": nrm(ks[12], (L, N_Q_HEADS), 1.0),
        "ssm_a_re": -0.5 + 0.01 * jax.random.normal(ks[13], (L, N_SSM_GROUPS, STATE), f32),
        "ssm_a_im": math.pi * n_idx + 0.01 * jax.random.normal(ks[14], (L, N_SSM_GROUPS, STATE), f32),
        "ssm_log_step": jax.random.uniform(ks[15], (L, N_SSM_GROUPS), f32, math.log(1e-3), math.log(1e-1)),
        "ssm_b_re": nrm(ks[16], (L, N_SSM_GROUPS, STATE, SSM_GROUP), (2 * SSM_GROUP) ** -0.5),
        "ssm_b_im": nrm(ks[17], (L, N_SSM_GROUPS, STATE, SSM_GROUP), (2 * SSM_GROUP) ** -0.5),
        "ssm_c_re": nrm(ks[18], (L, N_SSM_GROUPS, SSM_GROUP, STATE), (2 * STATE) ** -0.5),
        "ssm_c_im": nrm(ks[19], (L, N_SSM_GROUPS, SSM_GROUP, STATE), (2 * STATE) ** -0.5),
        "ssm_d": nrm(ks[20], (L, D_SSM), 1.0),
        "ssm_w_glu": nrm(ks[21], (L, D_SSM, D_SSM), D_SSM ** -0.5),
        "ssm_b_glu": nrm(ks[22], (L, D_SSM), 0.02),
        "g_attn_out": 1.0 + 0.05 * jax.random.normal(ks[23], (L, D_ATTN), f32),
        "g_ssm_out": 1.0 + 0.05 * jax.random.normal(ks[24], (L, D_SSM), f32),
        "w_out": nrm(ks[25], (L, D_MIX, D_MODEL), D_MIX ** -0.5),
        "g_pre_ff2": gain(ks[26]),
        "g_post_ff2": gain(ks[27]),
        "w2_gate": nrm(ks[28], (L, D_MODEL, D_FF), D_MODEL ** -0.5),
        "w2_up": nrm(ks[29], (L, D_MODEL, D_FF), D_MODEL ** -0.5),
        "w2_down": nrm(ks[30], (L, D_FF, D_MODEL), D_FF ** -0.5),
    }


def reference(x, c, w_ada, b_ada, g_pre_ff1, g_post_ff1, w1_gate, w1_up, w1_down,
              g_pre_mix, g_post_mix, w_in, attn_sinks, ssm_a_re, ssm_a_im, ssm_log_step,
              ssm_b_re, ssm_b_im, ssm_c_re, ssm_c_im, ssm_d, ssm_w_glu, ssm_b_glu,
              g_attn_out, g_ssm_out, w_out, g_pre_ff2, g_post_ff2, w2_gate, w2_up, w2_down):
    bsz, s = x.shape[0], x.shape[1]
    c_act = jax.nn.silu(c)
    for l in range(DEPTH):
        mod = c_act @ w_ada[l] + b_ada[l]
        (sh1, sc1, ga1, sh2, sc2, ga2, sh3, sc3, ga3) = jnp.split(mod, N_MOD, axis=-1)

        h = modulate(rms_norm(x, g_pre_ff1[l]), sh1, sc1)
        f = swiglu(h, w1_gate[l], w1_up[l], w1_down[l])
        x = x + 0.5 * ga1[:, None, :] * rms_norm(f, g_post_ff1[l])

        h = modulate(rms_norm(x, g_pre_mix[l]), sh2, sc2)
        proj = h @ w_in[l]
        q = proj[..., :D_ATTN].reshape(bsz, s, N_Q_HEADS, HEAD_DIM)
        k = proj[..., D_ATTN:D_ATTN + D_KV].reshape(bsz, s, N_KV_HEADS, HEAD_DIM)
        v = proj[..., D_ATTN + D_KV:D_ATTN + 2 * D_KV].reshape(bsz, s, N_KV_HEADS, HEAD_DIM)
        u = proj[..., D_ATTN + 2 * D_KV:]
        attn = sliding_window_attention(q, k, v, attn_sinks[l])
        ssm = s5_ssm(u, ssm_a_re[l], ssm_a_im[l], ssm_log_step[l], ssm_b_re[l], ssm_b_im[l],
                     ssm_c_re[l], ssm_c_im[l], ssm_d[l], ssm_w_glu[l], ssm_b_glu[l])
        mixed = jnp.concatenate([rms_norm(attn, g_attn_out[l]), rms_norm(ssm, g_ssm_out[l])], axis=-1)
        mixed = mixed @ w_out[l]
        x = x + ga2[:, None, :] * rms_norm(mixed, g_post_mix[l])

        h = modulate(rms_norm(x, g_pre_ff2[l]), sh3, sc3)
        f = swiglu(h, w2_gate[l], w2_up[l], w2_down[l])
        x = x + 0.5 * ga3[:, None, :] * rms_norm(f, g_post_ff2[l])
    return x
```

```python
import functools

import jax
import jax.numpy as jnp
from jax import lax
from jax.experimental import pallas as pl
from jax.experimental.pallas import tpu as pltpu

D_MODEL = 1024
HEAD_DIM = 64
D_ATTN = 512
N_Q_HEADS = 8
N_KV_HEADS = 2
Q_PER_KV = 4
BLOCK = 128
D_SSM = 512
SSM_GROUP = 16
N_SSM_GROUPS = 32
STATE = 64
D_KV = 128
D_IN = 1280
D_FF = 2816
N_MOD = 9
EPS = 1e-6

SSM_CHUNK = 128
SSM_HALF_GROUPS = N_SSM_GROUPS // 2
SSM_HALF_STATE = SSM_HALF_GROUPS * STATE

FF_ROWS = 512
PROJ_ROWS = 512
VMEM_LIMIT = 56 * 1024 * 1024

F32 = jnp.float32
BF16 = jnp.bfloat16


def _rms(x, g):
    return x * lax.rsqrt(jnp.mean(x * x, axis=-1, keepdims=True) + EPS) * g


def _const_spec(shape):
    nd = len(shape)
    return pl.BlockSpec(shape, lambda *_: (0,) * nd, pipeline_mode=pl.Buffered(1))


def _ada_kernel(c_ref, w_ref, b_ref, o_ref):
    c = c_ref[...]
    ca = (c * jax.nn.sigmoid(c)).astype(BF16)
    o_ref[...] = jnp.dot(ca, w_ref[...].astype(BF16), preferred_element_type=F32) + b_ref[...]


def _ada(c, w, b):
    bsz, d = c.shape
    n = w.shape[1]
    tn = 1024
    return pl.pallas_call(
        _ada_kernel,
        grid=(n // tn,),
        in_specs=[pl.BlockSpec((bsz, d), lambda j: (0, 0)),
                  pl.BlockSpec((d, tn), lambda j: (0, j)),
                  pl.BlockSpec((1, tn), lambda j: (0, j))],
        out_specs=pl.BlockSpec((bsz, tn), lambda j: (0, j)),
        out_shape=jax.ShapeDtypeStruct((bsz, n), F32),
        compiler_params=pltpu.CompilerParams(dimension_semantics=("arbitrary",)),
        name="ada",
    )(c, w, b.reshape(1, n))


def _ff_kernel(x_ref, mod_ref, gpre_ref, gpost_ref, wg_ref, wu_ref, wd_ref, o_ref, *, base):
    x = x_ref[0]
    shift = mod_ref[0, base:base + 1, :]
    scale = mod_ref[0, base + 1:base + 2, :]
    gate = mod_ref[0, base + 2:base + 3, :]
    h = (_rms(x, gpre_ref[...]) * (1.0 + scale) + shift).astype(BF16)
    g = jnp.dot(h, wg_ref[...], preferred_element_type=F32)
    u = jnp.dot(h, wu_ref[...], preferred_element_type=F32)
    a = (g * jax.nn.sigmoid(g) * u).astype(BF16)
    f = jnp.dot(a, wd_ref[...], preferred_element_type=F32)
    o_ref[0] = x + (0.5 * gate) * _rms(f, gpost_ref[...])


def _ff(x, mod, base, g_pre, g_post, wg, wu, wd):
    bsz, s, d = x.shape
    tm = FF_ROWS
    return pl.pallas_call(
        functools.partial(_ff_kernel, base=base),
        grid=(bsz, s // tm),
        in_specs=[pl.BlockSpec((1, tm, d), lambda b, i: (b, i, 0)),
                  pl.BlockSpec((1, N_MOD, d), lambda b, i: (b, 0, 0)),
                  _const_spec((1, d)), _const_spec((1, d)),
                  _const_spec(wg.shape), _const_spec(wu.shape), _const_spec(wd.shape)],
        out_specs=pl.BlockSpec((1, tm, d), lambda b, i: (b, i, 0)),
        out_shape=jax.ShapeDtypeStruct(x.shape, F32),
        compiler_params=pltpu.CompilerParams(
            dimension_semantics=("arbitrary", "arbitrary"), vmem_limit_bytes=VMEM_LIMIT),
        name="ff",
    )(x, mod, g_pre.reshape(1, d), g_post.reshape(1, d), wg, wu, wd)


def _inproj_kernel(x_ref, mod_ref, g_ref, w_ref, q_ref, kv_ref, u_ref):
    x = x_ref[0]
    shift = mod_ref[0, 3:4, :]
    scale = mod_ref[0, 4:5, :]
    h = (_rms(x, g_ref[...]) * (1.0 + scale) + shift).astype(BF16)
    p = jnp.dot(h, w_ref[...], preferred_element_type=F32)
    q_ref[0] = (p[:, :D_ATTN] * (HEAD_DIM ** -0.5)).astype(BF16)
    kv_ref[0] = p[:, D_ATTN:D_ATTN + 2 * D_KV].astype(BF16)
    u_ref[0] = p[:, D_ATTN + 2 * D_KV:]


def _inproj(x, mod, g, w):
    bsz, s, d = x.shape
    tm = PROJ_ROWS
    row = lambda b, i: (b, i, 0)
    return pl.pallas_call(
        _inproj_kernel,
        grid=(bsz, s // tm),
        in_specs=[pl.BlockSpec((1, tm, d), row),
                  pl.BlockSpec((1, N_MOD, d), lambda b, i: (b, 0, 0)),
                  _const_spec((1, d)), _const_spec(w.shape)],
        out_specs=[pl.BlockSpec((1, tm, D_ATTN), row),
                   pl.BlockSpec((1, tm, 2 * D_KV), row),
                   pl.BlockSpec((1, tm, D_SSM), row)],
        out_shape=[jax.ShapeDtypeStruct((bsz, s, D_ATTN), BF16),
                   jax.ShapeDtypeStruct((bsz, s, 2 * D_KV), BF16),
                   jax.ShapeDtypeStruct((bsz, s, D_SSM), F32)],
        compiler_params=pltpu.CompilerParams(
            dimension_semantics=("arbitrary", "arbitrary"), vmem_limit_bytes=VMEM_LIMIT),
        name="inproj",
    )(x, mod, g.reshape(1, d), w)


def _attn_kernel(sink_ref, q_ref, kvc_ref, kvp_ref, bias_ref, g_ref, o_ref):
    first = (pl.program_id(1) == 0).astype(jnp.int32)
    q = q_ref[0]
    kv = jnp.concatenate([kvp_ref[0], kvc_ref[0]], axis=0)
    outs = []
    for hq in range(N_Q_HEADS):
        hk = hq // Q_PER_KV
        qh = q[:, hq * HEAD_DIM:(hq + 1) * HEAD_DIM]
        kh = kv[:, hk * HEAD_DIM:(hk + 1) * HEAD_DIM]
        vh = kv[:, D_KV + hk * HEAD_DIM:D_KV + (hk + 1) * HEAD_DIM]
        s = lax.dot_general(qh, kh, (((1,), (1,)), ((), ())), preferred_element_type=F32)
        s = s + bias_ref[first, hq]
        sink = sink_ref[hq]
        m = jnp.maximum(jnp.max(s, axis=-1, keepdims=True), sink)
        e = jnp.exp(s - m)
        den = jnp.sum(e, axis=-1, keepdims=True) + jnp.exp(sink - m)
        p = (e * (1.0 / den)).astype(BF16)
        outs.append(jnp.dot(p, vh, preferred_element_type=F32))
    o = jnp.concatenate(outs, axis=1)
    o_ref[0] = _rms(o, g_ref[...]).astype(BF16)


def _attn(q, kv, bias, sinks, g):
    bsz, s, _ = q.shape
    nb = s // BLOCK
    return pl.pallas_call(
        _attn_kernel,
        grid=(bsz, nb),
        in_specs=[pl.BlockSpec(memory_space=pltpu.SMEM),
                  pl.BlockSpec((1, BLOCK, D_ATTN), lambda b, n: (b, n, 0)),
                  pl.BlockSpec((1, BLOCK, 2 * D_KV), lambda b, n: (b, n, 0)),
                  pl.BlockSpec((1, BLOCK, 2 * D_KV), lambda b, n: (b, jnp.maximum(n - 1, 0), 0)),
                  _const_spec(bias.shape), _const_spec((1, D_ATTN))],
        out_specs=pl.BlockSpec((1, BLOCK, D_ATTN), lambda b, n: (b, n, 0)),
        out_shape=jax.ShapeDtypeStruct((bsz, s, D_ATTN), BF16),
        compiler_params=pltpu.CompilerParams(
            dimension_semantics=("arbitrary", "arbitrary"), vmem_limit_bytes=VMEM_LIMIT),
        name="attn",
    )(sinks, q, kv, kv, bias, g.reshape(1, D_ATTN))


def _attn_bias():
    qi = jnp.arange(BLOCK)[:, None]
    kj = jnp.arange(2 * BLOCK)[None, :]
    dist = qi + BLOCK - kj
    valid = (dist >= 0) & (dist < BLOCK)
    slopes = jnp.exp2(-8.0 * jnp.arange(1, N_Q_HEADS + 1, dtype=F32) / N_Q_HEADS)
    bias = -slopes[:, None, None] * dist.astype(F32)[None]
    rest = jnp.where(valid[None], bias, -jnp.inf)
    first = jnp.where((valid & (kj >= BLOCK))[None], bias, -jnp.inf)
    return jnp.stack([rest, first])


def _gelu_tanh(x):
    return 0.5 * x * (1.0 + jnp.tanh(0.7978845608028654 * (x + 0.044715 * (x * x * x))))


def _ssm_kernel(u_ref, bm_ref, cm_ref, tn_ref, tp_ref, a_ref, tri_ref, d_ref, wglu_ref,
                bglu_ref, g_ref, o_ref, carry_ref):
    @pl.when(pl.program_id(1) == 0)
    def _():
        carry_ref[...] = jnp.zeros_like(carry_ref)

    hs = SSM_HALF_STATE
    u = u_ref[0]
    ub = u.astype(BF16)
    tri = tri_ref[...]
    ys = []
    for h in range(2):
        re = slice(2 * h * hs, (2 * h + 1) * hs)
        im = slice((2 * h + 1) * hs, (2 * h + 2) * hs)
        bu = jnp.dot(ub[:, h * 256:(h + 1) * 256], bm_ref[h], preferred_element_type=F32)
        br, bi = bu[:, :hs], bu[:, hs:]
        nr, ni = tn_ref[:, re], tn_ref[:, im]
        z = jnp.concatenate([br * nr - bi * ni, br * ni + bi * nr], axis=1).astype(BF16)
        zt = jnp.dot(tri, z, preferred_element_type=F32)
        cr, ci = carry_ref[0:1, re], carry_ref[0:1, im]
        ar, ai = a_ref[:, re], a_ref[:, im]
        wr = zt[:, :hs] + (ar * cr - ai * ci)
        wi = zt[:, hs:] + (ar * ci + ai * cr)
        pr, pi = tp_ref[:, re], tp_ref[:, im]
        xr = wr * pr - wi * pi
        xi = wr * pi + wi * pr
        carry_ref[0:1, re] = xr[SSM_CHUNK - 1:SSM_CHUNK, :]
        carry_ref[0:1, im] = xi[SSM_CHUNK - 1:SSM_CHUNK, :]
        xs = jnp.concatenate([xr, xi], axis=1).astype(BF16)
        ys.append(jnp.dot(xs, cm_ref[h], preferred_element_type=F32))
    y = jnp.concatenate(ys, axis=1) + u * d_ref[...]
    y = _gelu_tanh(y)
    gl = jnp.dot(y.astype(BF16), wglu_ref[...], preferred_element_type=F32) + bglu_ref[...]
    y = y * jax.nn.sigmoid(gl)
    o_ref[0] = _rms(y, g_ref[...]).astype(BF16)


def _ssm(u, bm, cm, tn, tp, a1, tri, d_skip, w_glu, b_glu, g):
    bsz, s, _ = u.shape
    L = SSM_CHUNK
    row = lambda b, n: (b, n, 0)
    consts = [bm, cm, tn, tp, a1, tri, d_skip.reshape(1, D_SSM), w_glu,
              b_glu.reshape(1, D_SSM), g.reshape(1, D_SSM)]
    return pl.pallas_call(
        _ssm_kernel,
        grid=(bsz, s // L),
        in_specs=[pl.BlockSpec((1, L, D_SSM), row)] + [_const_spec(c.shape) for c in consts],
        out_specs=pl.BlockSpec((1, L, D_SSM), row),
        out_shape=jax.ShapeDtypeStruct((bsz, s, D_SSM), BF16),
        scratch_shapes=[pltpu.VMEM((8, 4 * SSM_HALF_STATE), F32)],
        compiler_params=pltpu.CompilerParams(
            dimension_semantics=("arbitrary", "arbitrary"), vmem_limit_bytes=VMEM_LIMIT),
        name="ssm",
    )(u, *consts)


def _ssm_params(a_re, a_im, log_step, b_re, b_im, c_re, c_im):
    L = SSM_CHUNK
    hg = SSM_HALF_GROUPS
    dt = jnp.exp(log_step.astype(F32))[:, None]
    lam = lax.complex(a_re.astype(F32), a_im.astype(F32))
    lam_bar = jnp.exp(lam * dt)
    b_bar = ((lam_bar - 1.0) / lam)[:, :, None] * lax.complex(b_re.astype(F32), b_im.astype(F32))

    def cols(z):
        lead = z.shape[:-2]
        z = z.reshape(lead + (2, hg * STATE))
        return jnp.concatenate([jnp.real(z), jnp.imag(z)], axis=-1).reshape(lead + (4 * hg * STATE,))

    k = jnp.arange(L, dtype=F32)[:, None, None]
    log_mag = (a_re.astype(F32) * dt)[None]
    phase = (a_im.astype(F32) * dt)[None]
    pos = lax.complex(jnp.exp(k * log_mag) * jnp.cos(k * phase), jnp.exp(k * log_mag) * jnp.sin(k * phase))
    neg = lax.complex(jnp.exp(-k * log_mag) * jnp.cos(k * phase), -jnp.exp(-k * log_mag) * jnp.sin(k * phase))
    tp = cols(pos)
    tn = cols(neg)
    a1 = cols(lam_bar[None])

    eye = jnp.eye(hg, dtype=F32)
    bb = b_bar.reshape(2, hg, STATE, SSM_GROUP)
    bre = jnp.einsum("xgph,gk->xghkp", jnp.real(bb), eye).reshape(2, hg * SSM_GROUP, hg * STATE)
    bim = jnp.einsum("xgph,gk->xghkp", jnp.imag(bb), eye).reshape(2, hg * SSM_GROUP, hg * STATE)
    bm = jnp.concatenate([bre, bim], axis=-1).astype(BF16)

    cc_re = c_re.astype(F32).reshape(2, hg, SSM_GROUP, STATE)
    cc_im = c_im.astype(F32).reshape(2, hg, SSM_GROUP, STATE)
    cre = jnp.einsum("xghp,gk->xgpkh", cc_re, eye).reshape(2, hg * STATE, hg * SSM_GROUP)
    cim = jnp.einsum("xghp,gk->xgpkh", cc_im, eye).reshape(2, hg * STATE, hg * SSM_GROUP)
    cm = jnp.concatenate([cre, -cim], axis=1).astype(BF16)
    tri = jnp.tril(jnp.ones((L, L), F32)).astype(BF16)
    return bm, cm, tn, tp, a1, tri


def _outproj_kernel(x_ref, at_ref, ss_ref, mod_ref, g_ref, w_ref, o_ref):
    x = x_ref[0]
    gate = mod_ref[0, 5:6, :]
    mixed = (jnp.dot(at_ref[0], w_ref[0:D_ATTN, :], preferred_element_type=F32)
             + jnp.dot(ss_ref[0], w_ref[D_ATTN:, :], preferred_element_type=F32))
    o_ref[0] = x + gate * _rms(mixed, g_ref[...])


def _outproj(x, attn, ssm, mod, g, w):
    bsz, s, d = x.shape
    tm = PROJ_ROWS
    row = lambda b, i: (b, i, 0)
    return pl.pallas_call(
        _outproj_kernel,
        grid=(bsz, s // tm),
        in_specs=[pl.BlockSpec((1, tm, d), row),
                  pl.BlockSpec((1, tm, D_ATTN), row),
                  pl.BlockSpec((1, tm, D_SSM), row),
                  pl.BlockSpec((1, N_MOD, d), lambda b, i: (b, 0, 0)),
                  _const_spec((1, d)), _const_spec(w.shape)],
        out_specs=pl.BlockSpec((1, tm, d), row),
        out_shape=jax.ShapeDtypeStruct(x.shape, F32),
        compiler_params=pltpu.CompilerParams(
            dimension_semantics=("arbitrary", "arbitrary"), vmem_limit_bytes=VMEM_LIMIT),
        name="outproj",
    )(x, attn, ssm, mod, g.reshape(1, d), w)


def kernel(x, c, w_ada, b_ada, g_pre_ff1, g_post_ff1, w1_gate, w1_up, w1_down, g_pre_mix, g_post_mix, w_in, attn_sinks, ssm_a_re, ssm_a_im, ssm_log_step, ssm_b_re, ssm_b_im, ssm_c_re, ssm_c_im, ssm_d, ssm_w_glu, ssm_b_glu, g_attn_out, g_ssm_out, w_out, g_pre_ff2, g_post_ff2, w2_gate, w2_up, w2_down):
    bsz = x.shape[0]
    bias = _attn_bias()
    for l in range(w_ada.shape[0]):
        mod = _ada(c, w_ada[l], b_ada[l]).reshape(bsz, N_MOD, D_MODEL)
        x = _ff(x, mod, 0, g_pre_ff1[l], g_post_ff1[l],
                w1_gate[l].astype(BF16), w1_up[l].astype(BF16), w1_down[l].astype(BF16))
        q, kv, u = _inproj(x, mod, g_pre_mix[l], w_in[l].astype(BF16))
        attn = _attn(q, kv, bias, attn_sinks[l].astype(F32), g_attn_out[l])
        sp = _ssm_params(ssm_a_re[l], ssm_a_im[l], ssm_log_step[l], ssm_b_re[l], ssm_b_im[l],
                         ssm_c_re[l], ssm_c_im[l])
        ssm = _ssm(u, *sp, ssm_d[l], ssm_w_glu[l].astype(BF16), ssm_b_glu[l], g_ssm_out[l])
        x = _outproj(x, attn, ssm, mod, g_post_mix[l], w_out[l].astype(BF16))
        x = _ff(x, mod, 6, g_pre_ff2[l], g_post_ff2[l],
                w2_gate[l].astype(BF16), w2_up[l].astype(BF16), w2_down[l].astype(BF16))
    return x
```

```python
import functools

import jax
import jax.numpy as jnp
from jax import lax
from jax.experimental import pallas as pl
from jax.experimental.pallas import tpu as pltpu

D_MODEL = 1024
HEAD_DIM = 64
D_ATTN = 512
N_Q_HEADS = 8
N_KV_HEADS = 2
Q_PER_KV = 4
BLOCK = 128
D_SSM = 512
SSM_GROUP = 16
N_SSM_GROUPS = 32
STATE = 64
D_KV = 128
D_IN = 1280
D_FF = 2816
N_MOD = 9
EPS = 1e-6

SSM_CHUNK = 128
SSM_HALF_GROUPS = N_SSM_GROUPS // 2
SSM_HALF_STATE = SSM_HALF_GROUPS * STATE

FF_ROWS = 512
PROJ_ROWS = 512
ATTN_ROWS = 512
SSM_ROWS = 512
VMEM_LIMIT = 56 * 1024 * 1024

F32 = jnp.float32
BF16 = jnp.bfloat16


def _rms(x, g):
    return x * lax.rsqrt(jnp.mean(x * x, axis=-1, keepdims=True) + EPS) * g


def _const_spec(shape):
    nd = len(shape)
    return pl.BlockSpec(shape, lambda *_: (0,) * nd, pipeline_mode=pl.Buffered(1))


def _ada_kernel(c_ref, w_ref, b_ref, o_ref):
    c = c_ref[...]
    ca = (c * jax.nn.sigmoid(c)).astype(BF16)
    o_ref[...] = jnp.dot(ca, w_ref[...].astype(BF16), preferred_element_type=F32) + b_ref[...]


def _ada(c, w, b):
    bsz, d = c.shape
    n = w.shape[1]
    tn = 1024
    return pl.pallas_call(
        _ada_kernel,
        grid=(n // tn,),
        in_specs=[pl.BlockSpec((bsz, d), lambda j: (0, 0)),
                  pl.BlockSpec((d, tn), lambda j: (0, j)),
                  pl.BlockSpec((1, tn), lambda j: (0, j))],
        out_specs=pl.BlockSpec((bsz, tn), lambda j: (0, j)),
        out_shape=jax.ShapeDtypeStruct((bsz, n), F32),
        compiler_params=pltpu.CompilerParams(dimension_semantics=("arbitrary",)),
        name="ada",
    )(c, w, b.reshape(1, n))


def _ff_kernel(x_ref, mod_ref, gpre_ref, gpost_ref, wg_ref, wu_ref, wd_ref, o_ref, *, base):
    x = x_ref[0]
    shift = mod_ref[0, base:base + 1, :]
    scale = mod_ref[0, base + 1:base + 2, :]
    gate = mod_ref[0, base + 2:base + 3, :]
    h = (_rms(x, gpre_ref[...]) * (1.0 + scale) + shift).astype(BF16)
    g = jnp.dot(h, wg_ref[...], preferred_element_type=F32)
    u = jnp.dot(h, wu_ref[...], preferred_element_type=F32)
    a = (g * jax.nn.sigmoid(g) * u).astype(BF16)
    f = jnp.dot(a, wd_ref[...], preferred_element_type=F32)
    o_ref[0] = x + (0.5 * gate) * _rms(f, gpost_ref[...])


def _ff(x, mod, base, g_pre, g_post, wg, wu, wd):
    bsz, s, d = x.shape
    tm = FF_ROWS
    return pl.pallas_call(
        functools.partial(_ff_kernel, base=base),
        grid=(bsz, s // tm),
        in_specs=[pl.BlockSpec((1, tm, d), lambda b, i: (b, i, 0)),
                  pl.BlockSpec((1, N_MOD, d), lambda b, i: (b, 0, 0)),
                  _const_spec((1, d)), _const_spec((1, d)),
                  _const_spec(wg.shape), _const_spec(wu.shape), _const_spec(wd.shape)],
        out_specs=pl.BlockSpec((1, tm, d), lambda b, i: (b, i, 0)),
        out_shape=jax.ShapeDtypeStruct(x.shape, F32),
        compiler_params=pltpu.CompilerParams(
            dimension_semantics=("arbitrary", "arbitrary"), vmem_limit_bytes=VMEM_LIMIT),
        name="ff",
    )(x, mod, g_pre.reshape(1, d), g_post.reshape(1, d), wg, wu, wd)


def _inproj_kernel(x_ref, mod_ref, g_ref, w_ref, q_ref, kv_ref, u_ref):
    x = x_ref[0]
    shift = mod_ref[0, 3:4, :]
    scale = mod_ref[0, 4:5, :]
    h = (_rms(x, g_ref[...]) * (1.0 + scale) + shift).astype(BF16)
    p = jnp.dot(h, w_ref[...], preferred_element_type=F32)
    q_ref[0] = (p[:, :D_ATTN] * (LOG2E * HEAD_DIM ** -0.5)).astype(BF16)
    kv_ref[0] = p[:, D_ATTN:D_ATTN + 2 * D_KV].astype(BF16)
    u_ref[0] = p[:, D_ATTN + 2 * D_KV:]


def _inproj(x, mod, g, w):
    bsz, s, d = x.shape
    tm = PROJ_ROWS
    row = lambda b, i: (b, i, 0)
    return pl.pallas_call(
        _inproj_kernel,
        grid=(bsz, s // tm),
        in_specs=[pl.BlockSpec((1, tm, d), row),
                  pl.BlockSpec((1, N_MOD, d), lambda b, i: (b, 0, 0)),
                  _const_spec((1, d)), _const_spec(w.shape)],
        out_specs=[pl.BlockSpec((1, tm, D_ATTN), row),
                   pl.BlockSpec((1, tm, 2 * D_KV), row),
                   pl.BlockSpec((1, tm, D_SSM), row)],
        out_shape=[jax.ShapeDtypeStruct((bsz, s, D_ATTN), BF16),
                   jax.ShapeDtypeStruct((bsz, s, 2 * D_KV), BF16),
                   jax.ShapeDtypeStruct((bsz, s, D_SSM), F32)],
        compiler_params=pltpu.CompilerParams(
            dimension_semantics=("arbitrary", "arbitrary"), vmem_limit_bytes=VMEM_LIMIT),
        name="inproj",
    )(x, mod, g.reshape(1, d), w)


ATTN_GROUPS = ((0, 2, 5, 7), (1, 3, 4, 6))
LOG2E = 1.4426950408889634


def _swap_halves(x):
    return jnp.concatenate([x[:, HEAD_DIM:], x[:, :HEAD_DIM]], axis=1)


def _attn_kernel(q_ref, kvc_ref, kvp_ref, bias_ref, sink_ref, g_ref, o_ref):
    first = (pl.program_id(1) == 0).astype(jnp.int32)
    low = lax.broadcasted_iota(jnp.int32, (BLOCK, 2 * HEAD_DIM), 1) < HEAD_DIM
    ones = jnp.ones((2 * BLOCK, 2 * HEAD_DIM), BF16)
    nblk = ATTN_ROWS // BLOCK
    blocks = [kvp_ref[0]] + [kvc_ref[0, j * BLOCK:(j + 1) * BLOCK, :] for j in range(nblk)]
    keys = [(b[:, :D_KV], _swap_halves(b[:, :D_KV])) for b in blocks]
    vals = [(b[:, D_KV:], _swap_halves(b[:, D_KV:])) for b in blocks]
    for j in range(nblk):
        rows = slice(j * BLOCK, (j + 1) * BLOCK)
        q = q_ref[0, rows, :]
        res = {}
        for grp, heads in enumerate(ATTN_GROUPS):
            kx = jnp.concatenate([keys[j][grp], keys[j + 1][grp]], axis=0)
            vx = jnp.concatenate([vals[j][grp], vals[j + 1][grp]], axis=0)
            qs = []
            for h in heads:
                qi = q[:, (h // 2) * 128:(h // 2 + 1) * 128]
                qs.append(jnp.where(low if h % 2 == 0 else ~low, qi, jnp.zeros_like(qi)))
            q4 = jnp.concatenate(qs, axis=0)
            s = lax.dot_general(q4, kx, (((1,), (1,)), ((), ())), preferred_element_type=F32)
            s = s + (bias_ref[first, grp] if j == 0 else bias_ref[0, grp])
            m = jnp.max(s, axis=-1, keepdims=True)
            e = jnp.exp2(s - m).astype(BF16)
            o = jnp.dot(e, jnp.concatenate([vx, ones], axis=1), preferred_element_type=F32)
            den = o[:, 2 * HEAD_DIM:] + jnp.exp2(sink_ref[grp] - m)
            r = o[:, :2 * HEAD_DIM] * (1.0 / den)
            for n, h in enumerate(heads):
                res[h] = r[n * BLOCK:(n + 1) * BLOCK, :]
        o = jnp.concatenate([jnp.where(low, res[2 * i], res[2 * i + 1])
                             for i in range(N_Q_HEADS // 2)], axis=1)
        o_ref[0, rows, :] = _rms(o, g_ref[...]).astype(BF16)


def _attn(q, kv, bias, sink_rows, g):
    bsz, s, _ = q.shape
    tm = ATTN_ROWS
    per = tm // BLOCK
    return pl.pallas_call(
        _attn_kernel,
        grid=(bsz, s // tm),
        in_specs=[pl.BlockSpec((1, tm, D_ATTN), lambda b, n: (b, n, 0)),
                  pl.BlockSpec((1, tm, 2 * D_KV), lambda b, n: (b, n, 0)),
                  pl.BlockSpec((1, BLOCK, 2 * D_KV),
                               lambda b, n: (b, jnp.maximum(n * per - 1, 0), 0)),
                  _const_spec(bias.shape), _const_spec(sink_rows.shape),
                  _const_spec((1, D_ATTN))],
        out_specs=pl.BlockSpec((1, tm, D_ATTN), lambda b, n: (b, n, 0)),
        out_shape=jax.ShapeDtypeStruct((bsz, s, D_ATTN), BF16),
        compiler_params=pltpu.CompilerParams(
            dimension_semantics=("arbitrary", "arbitrary"), vmem_limit_bytes=VMEM_LIMIT),
        name="attn",
    )(q, kv, kv, bias, sink_rows, g.reshape(1, D_ATTN))


def _attn_tables(sinks):
    qi = jnp.arange(BLOCK)[:, None]
    kj = jnp.arange(2 * BLOCK)[None, :]
    dist = qi + BLOCK - kj
    valid = (dist >= 0) & (dist < BLOCK)
    slopes = jnp.exp2(-8.0 * jnp.arange(1, N_Q_HEADS + 1, dtype=F32) / N_Q_HEADS)
    bias = -(LOG2E * slopes)[:, None, None] * dist.astype(F32)[None]
    rest = jnp.where(valid[None], bias, -jnp.inf)
    first = jnp.where((valid & (kj >= BLOCK))[None], bias, -jnp.inf)
    order = jnp.array(ATTN_GROUPS)
    table = jnp.stack([rest, first])[:, order]
    table = table.reshape(2, 2, Q_PER_KV * BLOCK, 2 * BLOCK)
    sink_rows = jnp.broadcast_to((LOG2E * sinks.astype(F32))[order][:, :, None, None],
                                 (2, Q_PER_KV, BLOCK, 2 * HEAD_DIM))
    return table, sink_rows.reshape(2, Q_PER_KV * BLOCK, 2 * HEAD_DIM)


def _gelu_tanh(x):
    return 0.5 * x * (1.0 + jnp.tanh(0.7978845608028654 * (x + 0.044715 * (x * x * x))))


def _ssm_kernel(u_ref, bm_ref, cm_ref, tn_ref, tp_ref, a_ref, tri_ref, d_ref, wglu_ref,
                bglu_ref, g_ref, o_ref, carry_ref):
    @pl.when(pl.program_id(1) == 0)
    def _():
        carry_ref[...] = jnp.zeros_like(carry_ref)

    hs = SSM_HALF_STATE
    L = SSM_CHUNK
    u = u_ref[0]
    ub = u.astype(BF16)
    tri = tri_ref[...]
    ys = []
    for h in range(2):
        re = slice(2 * h * hs, (2 * h + 1) * hs)
        im = slice((2 * h + 1) * hs, (2 * h + 2) * hs)
        bu = jnp.dot(ub[:, h * 256:(h + 1) * 256], bm_ref[h], preferred_element_type=F32)
        nr, ni = tn_ref[:, re], tn_ref[:, im]
        pr, pi = tp_ref[:, re], tp_ref[:, im]
        ar, ai = a_ref[:, re], a_ref[:, im]
        cr, ci = carry_ref[0:1, re], carry_ref[0:1, im]
        xs = []
        for c in range(SSM_ROWS // L):
            br, bi = bu[c * L:(c + 1) * L, :hs], bu[c * L:(c + 1) * L, hs:]
            z = jnp.concatenate([br * nr - bi * ni, br * ni + bi * nr], axis=1).astype(BF16)
            zt = jnp.dot(tri, z, preferred_element_type=F32)
            wr = zt[:, :hs] + (ar * cr - ai * ci)
            wi = zt[:, hs:] + (ar * ci + ai * cr)
            xr = wr * pr - wi * pi
            xi = wr * pi + wi * pr
            cr, ci = xr[L - 1:L, :], xi[L - 1:L, :]
            xs.append(jnp.concatenate([xr, xi], axis=1).astype(BF16))
        carry_ref[0:1, re] = cr
        carry_ref[0:1, im] = ci
        ys.append(jnp.dot(jnp.concatenate(xs, axis=0), cm_ref[h], preferred_element_type=F32))
    y = jnp.concatenate(ys, axis=1) + u * d_ref[...]
    y = _gelu_tanh(y)
    gl = jnp.dot(y.astype(BF16), wglu_ref[...], preferred_element_type=F32) + bglu_ref[...]
    y = y * jax.nn.sigmoid(gl)
    o_ref[0] = _rms(y, g_ref[...]).astype(BF16)


def _ssm(u, bm, cm, tn, tp, a1, tri, d_skip, w_glu, b_glu, g):
    bsz, s, _ = u.shape
    L = SSM_ROWS
    row = lambda b, n: (b, n, 0)
    consts = [bm, cm, tn, tp, a1, tri, d_skip.reshape(1, D_SSM), w_glu,
              b_glu.reshape(1, D_SSM), g.reshape(1, D_SSM)]
    return pl.pallas_call(
        _ssm_kernel,
        grid=(bsz, s // L),
        in_specs=[pl.BlockSpec((1, L, D_SSM), row)] + [_const_spec(c.shape) for c in consts],
        out_specs=pl.BlockSpec((1, L, D_SSM), row),
        out_shape=jax.ShapeDtypeStruct((bsz, s, D_SSM), BF16),
        scratch_shapes=[pltpu.VMEM((8, 4 * SSM_HALF_STATE), F32)],
        compiler_params=pltpu.CompilerParams(
            dimension_semantics=("arbitrary", "arbitrary"), vmem_limit_bytes=VMEM_LIMIT),
        name="ssm",
    )(u, *consts)


def _ssm_params(a_re, a_im, log_step, b_re, b_im, c_re, c_im):
    L = SSM_CHUNK
    hg = SSM_HALF_GROUPS
    dt = jnp.exp(log_step.astype(F32))[:, None]
    lr, li = a_re.astype(F32), a_im.astype(F32)
    log_mag, phase = lr * dt, li * dt
    ar, ai = jnp.exp(log_mag) * jnp.cos(phase), jnp.exp(log_mag) * jnp.sin(phase)
    den = lr * lr + li * li
    qr = ((ar - 1.0) * lr + ai * li) / den
    qi = (ai * lr - (ar - 1.0) * li) / den
    br_, bi_ = b_re.astype(F32), b_im.astype(F32)
    bbar_re = qr[:, :, None] * br_ - qi[:, :, None] * bi_
    bbar_im = qr[:, :, None] * bi_ + qi[:, :, None] * br_

    def cols(zr, zi):
        lead = zr.shape[0]
        z = jnp.concatenate([zr.reshape(lead, 2, hg * STATE), zi.reshape(lead, 2, hg * STATE)],
                            axis=-1)
        return z.reshape(lead, 4 * hg * STATE)

    k = jnp.arange(L, dtype=F32)[:, None, None]
    mag_p, mag_n = jnp.exp(k * log_mag[None]), jnp.exp(-k * log_mag[None])
    cs, sn = jnp.cos(k * phase[None]), jnp.sin(k * phase[None])
    tp = cols(mag_p * cs, mag_p * sn)
    tn = cols(mag_n * cs, -mag_n * sn)
    a1 = cols(ar[None], ai[None])

    eye = jnp.eye(hg, dtype=F32)
    bre = jnp.einsum("xgph,gk->xghkp", bbar_re.reshape(2, hg, STATE, SSM_GROUP), eye)
    bim = jnp.einsum("xgph,gk->xghkp", bbar_im.reshape(2, hg, STATE, SSM_GROUP), eye)
    bre = bre.reshape(2, hg * SSM_GROUP, hg * STATE)
    bim = bim.reshape(2, hg * SSM_GROUP, hg * STATE)
    bm = jnp.concatenate([bre, bim], axis=-1).astype(BF16)

    cc_re = c_re.astype(F32).reshape(2, hg, SSM_GROUP, STATE)
    cc_im = c_im.astype(F32).reshape(2, hg, SSM_GROUP, STATE)
    cre = jnp.einsum("xghp,gk->xgpkh", cc_re, eye).reshape(2, hg * STATE, hg * SSM_GROUP)
    cim = jnp.einsum("xghp,gk->xgpkh", cc_im, eye).reshape(2, hg * STATE, hg * SSM_GROUP)
    cm = jnp.concatenate([cre, -cim], axis=1).astype(BF16)
    tri = jnp.tril(jnp.ones((L, L), F32)).astype(BF16)
    return bm, cm, tn, tp, a1, tri


def _outproj_kernel(x_ref, at_ref, ss_ref, mod_ref, g_ref, w_ref, o_ref):
    x = x_ref[0]
    gate = mod_ref[0, 5:6, :]
    mixed = (jnp.dot(at_ref[0], w_ref[0:D_ATTN, :], preferred_element_type=F32)
             + jnp.dot(ss_ref[0], w_ref[D_ATTN:, :], preferred_element_type=F32))
    o_ref[0] = x + gate * _rms(mixed, g_ref[...])


def _outproj(x, attn, ssm, mod, g, w):
    bsz, s, d = x.shape
    tm = PROJ_ROWS
    row = lambda b, i: (b, i, 0)
    return pl.pallas_call(
        _outproj_kernel,
        grid=(bsz, s // tm),
        in_specs=[pl.BlockSpec((1, tm, d), row),
                  pl.BlockSpec((1, tm, D_ATTN), row),
                  pl.BlockSpec((1, tm, D_SSM), row),
                  pl.BlockSpec((1, N_MOD, d), lambda b, i: (b, 0, 0)),
                  _const_spec((1, d)), _const_spec(w.shape)],
        out_specs=pl.BlockSpec((1, tm, d), row),
        out_shape=jax.ShapeDtypeStruct(x.shape, F32),
        compiler_params=pltpu.CompilerParams(
            dimension_semantics=("arbitrary", "arbitrary"), vmem_limit_bytes=VMEM_LIMIT),
        name="outproj",
    )(x, attn, ssm, mod, g.reshape(1, d), w)


def kernel(x, c, w_ada, b_ada, g_pre_ff1, g_post_ff1, w1_gate, w1_up, w1_down, g_pre_mix, g_post_mix, w_in, attn_sinks, ssm_a_re, ssm_a_im, ssm_log_step, ssm_b_re, ssm_b_im, ssm_c_re, ssm_c_im, ssm_d, ssm_w_glu, ssm_b_glu, g_attn_out, g_ssm_out, w_out, g_pre_ff2, g_post_ff2, w2_gate, w2_up, w2_down):
    bsz = x.shape[0]
    for l in range(w_ada.shape[0]):
        mod = _ada(c, w_ada[l], b_ada[l]).reshape(bsz, N_MOD, D_MODEL)
        x = _ff(x, mod, 0, g_pre_ff1[l], g_post_ff1[l],
                w1_gate[l].astype(BF16), w1_up[l].astype(BF16), w1_down[l].astype(BF16))
        q, kv, u = _inproj(x, mod, g_pre_mix[l], w_in[l].astype(BF16))
        attn = _attn(q, kv, *_attn_tables(attn_sinks[l]), g_attn_out[l])
        sp = _ssm_params(ssm_a_re[l], ssm_a_im[l], ssm_log_step[l], ssm_b_re[l], ssm_b_im[l],
                         ssm_c_re[l], ssm_c_im[l])
        ssm = _ssm(u, *sp, ssm_d[l], ssm_w_glu[l].astype(BF16), ssm_b_glu[l], g_ssm_out[l])
        x = _outproj(x, attn, ssm, mod, g_post_mix[l], w_out[l].astype(BF16))
        x = _ff(x, mod, 6, g_pre_ff2[l], g_post_ff2[l],
                w2_gate[l].astype(BF16), w2_up[l].astype(BF16), w2_down[l].astype(BF16))
    return x
```
